```python
import jax, jax.numpy as jnp
from jax import lax
import numpy as np

D_MODEL = 2048
BATCH = 1
SEQ = 8192
DEPTH = 2
DEC_BATCH = 128
DEC_SEQ = 1
PAST_LEN = 2048
PAGE_SIZE = 128

HEAD_DIM = 128
H_FOX = 8
D_FOX = H_FOX * HEAD_DIM
H_GLA = 4
DK_GLA = 64
DV_GLA = HEAD_DIM
GLA_RANK = 16
GLA_TAU = 16.0
H_GDN = 4
DK_GDN = HEAD_DIM
DV_GDN = HEAD_DIM
D_GDN_CONV = H_GDN * (2 * DK_GDN + DV_GDN)
CONV_W = 4
H_ALL = H_FOX + H_GLA + H_GDN
D_MIX = H_ALL * HEAD_DIM
Q_BLOCK = 128
CHUNK = 64
EPS = 1e-6
SEG_SIZES = (D_FOX, D_FOX, D_FOX, H_FOX,
             H_GLA * DK_GLA, H_GLA * DK_GLA, H_GLA * DV_GLA, GLA_RANK,
             D_GDN_CONV, H_GDN, H_GDN,
             D_MIX)
D_IN = 3 * D_FOX + H_FOX + 2 * H_GLA * DK_GLA + H_GLA * DV_GLA + GLA_RANK + D_GDN_CONV + 2 * H_GDN + D_MIX

kernel_name = 'hymba_fox_gla_gdn_decode_step'

F32 = jnp.float32


def _rms(x, g):
    xf = x.astype(F32)
    y = xf * lax.rsqrt(jnp.mean(xf * xf, axis=-1, keepdims=True) + EPS) * g.astype(F32)
    return y.astype(x.dtype)


def _l2n(x):
    xf = x.astype(F32)
    return xf * lax.rsqrt(jnp.sum(xf * xf, axis=-1, keepdims=True) + EPS)


def _split_cols(proj):
    idx = np.cumsum(SEG_SIZES)[:-1].tolist()
    return jnp.split(proj, idx, axis=-1)


def _gather_pages(pool, page_table):
    g = pool[page_table]
    return g.reshape((g.shape[0], g.shape[1] * g.shape[2]) + g.shape[3:])


def _chunk_pad(a, L, C):
    n = -(-L // C)
    pad = n * C - L
    a = jnp.pad(a, ((0, 0), (0, pad)) + ((0, 0),) * (a.ndim - 2))
    return jnp.moveaxis(a.reshape((a.shape[0], n, C) + a.shape[2:]), 1, 0)


def _unchunk(o, L):
    o = jnp.moveaxis(o, 0, 1)
    return o.reshape((o.shape[0], -1) + o.shape[3:])[:, :L]


def _fox_attend(q, Fq, q_pos, segs):
    B, Lq, H, Dh = q.shape
    qb = min(Q_BLOCK, Lq)
    nb = -(-Lq // qb)
    pad = nb * qb - Lq
    q_blk = jnp.moveaxis(jnp.pad(q, ((0, 0), (0, pad), (0, 0), (0, 0))).reshape(B, nb, qb, H, Dh), 1, 0)
    f_blk = jnp.moveaxis(jnp.pad(Fq, ((0, 0), (0, pad), (0, 0))).reshape(B, nb, qb, H), 1, 0)
    p_blk = jnp.pad(q_pos, (0, pad), mode='edge').reshape(nb, qb)
    scale = Dh ** -0.5
    offs = np.cumsum([s[0].shape[1] for s in segs])[:-1].tolist()

    def block(args):
        qi, fi, pi = args
        logits = []
        for k_s, _, F_s, pos_s in segs:
            s = jnp.einsum('bqhd,bkhd->bhqk', qi, k_s, preferred_element_type=F32) * scale
            s = s + (jnp.swapaxes(fi, 1, 2)[..., :, None] - jnp.swapaxes(F_s, 1, 2)[..., None, :])
            s = jnp.where(pos_s[None, :] <= pi[:, None], s, -jnp.inf)
            logits.append(s)
        p = jax.nn.softmax(jnp.concatenate(logits, axis=-1), axis=-1)
        ps = jnp.split(p, offs, axis=-1)
        return sum(jnp.einsum('bhqk,bkhd->bqhd', p_s.astype(seg[1].dtype), seg[1], preferred_element_type=F32)
                   for p_s, seg in zip(ps, segs))

    o = lax.map(block, (q_blk, f_blk, p_blk))
    return jnp.moveaxis(o, 0, 1).reshape(B, nb * qb, H, Dh)[:, :Lq]


def _gla_chunked(q, k, v, log_a, S0):
    B, L = q.shape[:2]
    C = min(CHUNK, L)
    tri = jnp.tril(jnp.ones((C, C), bool))[None, :, :, None, None]
    xs = tuple(_chunk_pad(a.astype(F32), L, C) for a in (q, k, v, log_a))

    def step(S, inp):
        qc, kc, vc, ac = inp
        b = jnp.cumsum(ac, axis=1)
        o_inter = jnp.einsum('bthk,bhkv->bthv', qc * jnp.exp(b), S)
        decay = jnp.exp(jnp.where(tri, b[:, :, None] - b[:, None, :], -jnp.inf))
        A = jnp.sum(qc[:, :, None] * kc[:, None, :] * decay, axis=-1)
        o_intra = jnp.einsum('btsh,bshv->bthv', A, vc)
        b_end = b[:, -1]
        S = jnp.exp(b_end)[..., None] * S + jnp.einsum('bshk,bshv->bhkv', kc * jnp.exp(b_end[:, None] - b), vc)
        return S, o_inter + o_intra

    S_end, o = lax.scan(step, S0.astype(F32), xs)
    return _unchunk(o, L), S_end


def _gdn_chunked(q, k, v, g, beta, S0):
    B, L = q.shape[:2]
    C = min(CHUNK, L)
    dv = v.shape[-1]
    tri = jnp.tril(jnp.ones((C, C), bool))
    strict = jnp.tril(jnp.ones((C, C), bool), -1)
    eye = jnp.eye(C, dtype=F32)
    xs = tuple(jnp.swapaxes(_chunk_pad(a.astype(F32), L, C), 2, 3) for a in (q, k, v, g, beta))

    def step(S, inp):
        qc, kc, vc, gc, bc = inp
        gam = jnp.cumsum(gc, axis=-1)
        decay = jnp.exp(jnp.where(tri, gam[..., :, None] - gam[..., None, :], -jnp.inf))
        A = jnp.where(strict, bc[..., :, None] * jnp.einsum('bhtk,bhsk->bhts', kc, kc) * decay, 0.0)
        rhs = jnp.concatenate([bc[..., None] * vc, (bc * jnp.exp(gam))[..., None] * kc], axis=-1)
        sol = lax.linalg.triangular_solve(eye + A, rhs, left_side=True, lower=True, unit_diagonal=True)
        u, w = sol[..., :dv], sol[..., dv:]
        delta = u - jnp.einsum('bhtk,bhkv->bhtv', w, S)
        attn = jnp.einsum('bhtk,bhsk->bhts', qc, kc) * decay
        o = jnp.einsum('bhtk,bhkv->bhtv', qc * jnp.exp(gam)[..., None], S) + jnp.einsum('bhts,bhsv->bhtv', attn, delta)
        g_end = gam[..., -1]
        S = jnp.exp(g_end)[..., None, None] * S + jnp.einsum('bhsk,bhsv->bhkv', kc * jnp.exp(g_end[..., None] - gam)[..., None], delta)
        return S, o

    S_end, o = lax.scan(step, S0.astype(F32), xs)
    return _unchunk(jnp.swapaxes(o, 2, 3), L), S_end


def _layer(x, past, norm_g, w_in, w_out, fox_qn, fox_kn, fox_fb, gla_wr, gla_br,
           gdn_conv_w, gdn_a_log, gdn_dt_bias, out_g):
    B, L, _ = x.shape
    h = _rms(x, norm_g)
    proj = jnp.einsum('bld,de->ble', h, w_in)
    fq, fk, fv, ff, gq, gk, gv, gr, dqkv, db, da, z = _split_cols(proj)
    if past is None:
        conv_buf = jnp.zeros((B, CONV_W - 1, D_GDN_CONV), dqkv.dtype)
        S_gla = jnp.zeros((B, H_GLA, DK_GLA, DV_GLA), F32)
        S_gdn = jnp.zeros((B, H_GDN, DK_GDN, DV_GDN), F32)
    else:
        k_past, v_past, lf_past, conv_buf, S_gla, S_gdn = past

    q = _rms(fq.reshape(B, L, H_FOX, HEAD_DIM), fox_qn)
    k = _rms(fk.reshape(B, L, H_FOX, HEAD_DIM), fox_kn)
    v = fv.reshape(B, L, H_FOX, HEAD_DIM)
    logf = jax.nn.log_sigmoid(ff.astype(F32) + fox_fb.astype(F32))
    if past is None:
        F = jnp.cumsum(logf, axis=1)
        q_pos = jnp.arange(L)
        segs = [(k, v, F, q_pos)]
        Fq = F
    else:
        P = k_past.shape[1]
        F = jnp.cumsum(jnp.concatenate([lf_past.astype(F32), logf], axis=1), axis=1)
        q_pos = P + jnp.arange(L)
        segs = [(k_past, v_past, F[:, :P], jnp.arange(P)), (k, v, F[:, P:], q_pos)]
        Fq = F[:, P:]
    o_fox = _fox_attend(q, Fq, q_pos, segs)

    log_a = jax.nn.log_sigmoid(jnp.einsum('blr,re->ble', gr, gla_wr).astype(F32) + gla_br.astype(F32)) / GLA_TAU
    o_gla, S_gla_new = _gla_chunked(gq.reshape(B, L, H_GLA, DK_GLA) * (DK_GLA ** -0.5),
                                    gk.reshape(B, L, H_GLA, DK_GLA),
                                    gv.reshape(B, L, H_GLA, DV_GLA),
                                    log_a.reshape(B, L, H_GLA, DK_GLA), S_gla)

    conv_in = jnp.concatenate([conv_buf.astype(dqkv.dtype), dqkv], axis=1)
    conv_out = jax.nn.silu(sum(conv_in[:, i:i + L] * gdn_conv_w[i] for i in range(CONV_W)))
    conv_new = conv_in[:, L:]
    cq, ck, cv = jnp.split(conv_out, [H_GDN * DK_GDN, 2 * H_GDN * DK_GDN], axis=-1)
    dq = _l2n(cq.reshape(B, L, H_GDN, DK_GDN)) * (DK_GDN ** -0.5)
    dk = _l2n(ck.reshape(B, L, H_GDN, DK_GDN))
    beta = jax.nn.sigmoid(db.astype(F32))
    g = -jnp.exp(gdn_a_log.astype(F32)) * jax.nn.softplus(da.astype(F32) + gdn_dt_bias.astype(F32))
    o_gdn, S_gdn_new = _gdn_chunked(dq, dk, cv.reshape(B, L, H_GDN, DV_GDN), g, beta, S_gdn)

    o = jnp.concatenate([o_fox, o_gla, o_gdn], axis=2)
    o = _rms(o, out_g.reshape(H_ALL, HEAD_DIM)).astype(x.dtype).reshape(B, L, D_MIX) * jax.nn.silu(z)
    y = x + jnp.einsum('ble,ed->bld', o, w_out)
    return y, (k, v, logf, conv_new, S_gla_new, S_gdn_new)


def _stack_field(states, i):
    return jnp.stack([s[i] for s in states])


def setup_inputs(seed: int = 0) -> dict:
    key = jax.random.key(seed)
    ks = jax.random.split(key, 24)
    n_pages = PAST_LEN // PAGE_SIZE
    n_used = DEC_BATCH * n_pages
    n_pool = n_used + (n_used + 3) // 4
    nrm = lambda k, s: jax.random.normal(k, s, F32)
    page_table = jax.random.permutation(ks[5], n_pool)[:n_used].reshape(DEC_BATCH, n_pages).astype(jnp.int32)
    dt = jnp.exp(jax.random.uniform(ks[17], (DEPTH, H_GDN), F32, np.log(1e-3), np.log(1e-1)))
    return {
        'x_prompt': nrm(ks[0], (BATCH, SEQ, D_MODEL)),
        'x_sample': nrm(ks[1], (DEC_BATCH, DEC_SEQ, D_MODEL)),
        'cache_k': nrm(ks[2], (DEPTH, n_pool, PAGE_SIZE, H_FOX, HEAD_DIM)),
        'cache_v': nrm(ks[3], (DEPTH, n_pool, PAGE_SIZE, H_FOX, HEAD_DIM)),
        'cache_logf': jax.nn.log_sigmoid(nrm(ks[4], (DEPTH, n_pool, PAGE_SIZE, H_FOX))),
        'state_conv': nrm(ks[6], (DEPTH, DEC_BATCH, CONV_W - 1, D_GDN_CONV)),
        'state_gla': 0.1 * nrm(ks[7], (DEPTH, DEC_BATCH, H_GLA, DK_GLA, DV_GLA)),
        'state_gdn': 0.1 * nrm(ks[8], (DEPTH, DEC_BATCH, H_GDN, DK_GDN, DV_GDN)),
        'page_table': page_table,
        'norm_g': 1.0 + 0.02 * nrm(ks[9], (DEPTH, D_MODEL)),
        'w_in': nrm(ks[10], (DEPTH, D_MODEL, D_IN)) * (D_MODEL ** -0.5),
        'w_out': nrm(ks[11], (DEPTH, D_MIX, D_MODEL)) * (D_MIX ** -0.5),
        'fox_qnorm_g': 1.0 + 0.02 * nrm(ks[12], (DEPTH, HEAD_DIM)),
        'fox_knorm_g': 1.0 + 0.02 * nrm(ks[13], (DEPTH, HEAD_DIM)),
        'fox_f_bias': 0.1 * nrm(ks[14], (DEPTH, H_FOX)),
        'gla_w_r2': nrm(ks[15], (DEPTH, GLA_RANK, H_GLA * DK_GLA)) * (GLA_RANK ** -0.5),
        'gla_b_r': 0.1 * nrm(ks[16], (DEPTH, H_GLA * DK_GLA)),
        'gdn_conv_w': nrm(ks[18], (DEPTH, CONV_W, D_GDN_CONV)) * (CONV_W ** -0.5),
        'gdn_a_log': jnp.log(jax.random.uniform(ks[19], (DEPTH, H_GDN), F32, 1.0, 16.0)),
        'gdn_dt_bias': dt + jnp.log(-jnp.expm1(-dt)),
        'out_norm_g': 1.0 + 0.02 * nrm(ks[20], (DEPTH, D_MIX)),
    }


def reference(x_prompt, x_sample, cache_k, cache_v, cache_logf, state_conv, state_gla, state_gdn,
              page_table, norm_g, w_in, w_out, fox_qnorm_g, fox_knorm_g, fox_f_bias, gla_w_r2, gla_b_r,
              gdn_conv_w, gdn_a_log, gdn_dt_bias, out_norm_g):
    Bp, Lp = x_prompt.shape[:2]
    n_pp = Lp // PAGE_SIZE
    yp, ys = x_prompt, x_sample
    new_p, new_s = [], []
    for l in range(DEPTH):
        w = (norm_g[l], w_in[l], w_out[l], fox_qnorm_g[l], fox_knorm_g[l], fox_f_bias[l], gla_w_r2[l],
             gla_b_r[l], gdn_conv_w[l], gdn_a_log[l], gdn_dt_bias[l], out_norm_g[l])
        yp, st_p = _layer(yp, None, *w)
        new_p.append(st_p)
        past = (_gather_pages(cache_k[l], page_table), _gather_pages(cache_v[l], page_table),
                _gather_pages(cache_logf[l], page_table), state_conv[l], state_gla[l], state_gdn[l])
        ys, st_s = _layer(ys, past, *w)
        new_s.append(st_s)
    p_k = _stack_field(new_p, 0).reshape(DEPTH, Bp, n_pp, PAGE_SIZE, H_FOX, HEAD_DIM)
    p_v = _stack_field(new_p, 1).reshape(DEPTH, Bp, n_pp, PAGE_SIZE, H_FOX, HEAD_DIM)
    p_logf = _stack_field(new_p, 2).reshape(DEPTH, Bp, n_pp, PAGE_SIZE, H_FOX)
    p_conv = _stack_field(new_p, 3)
    p_gla = _stack_field(new_p, 4)
    p_gdn = _stack_field(new_p, 5)
    s_k = _stack_field(new_s, 0)
    s_v = _stack_field(new_s, 1)
    s_logf = _stack_field(new_s, 2)
    s_conv = _stack_field(new_s, 3)
    s_gla = _stack_field(new_s, 4)
    s_gdn = _stack_field(new_s, 5)
    return (yp, ys, p_k, p_v, p_logf, p_conv, p_gla, p_gdn, s_k, s_v, s_logf, s_conv, s_gla, s_gdn)
```

```python
import functools
import math

import jax
import jax.numpy as jnp
import numpy as np
from jax import lax
from jax.experimental import pallas as pl
from jax.experimental.pallas import tpu as pltpu

F32 = jnp.float32
BF16 = jnp.bfloat16
HI = lax.Precision.HIGHEST
SDS = jax.ShapeDtypeStruct

EPS = 1e-6
D_MODEL = 2048
HEAD_DIM = 128
H_FOX = 8
D_FOX = H_FOX * HEAD_DIM
H_GLA = 4
DK_GLA = 64
D_GLA_K = H_GLA * DK_GLA
D_GLA_V = H_GLA * HEAD_DIM
GLA_RANK = 16
GLA_TAU = 16.0
H_GDN = 4
D_GDN = H_GDN * HEAD_DIM
CONV_W = 4
H_ALL = 16
D_MIX = H_ALL * HEAD_DIM
PAGE = 128
GDN_CHUNK = 64

LANES = 128
SUBLANES = 8
VMEM_LIMIT = 56 * 1024 * 1024

OFF_FQ, OFF_FK, OFF_FV = 0, 1024, 2048
OFF_GQK, OFF_GV = 3072, 3584
OFF_Z = 4096
OFF_DQ, OFF_DK, OFF_DV = 6144, 6656, 7168
OFF_SM = 7680
N_PACK = 8192
SM_FF, SM_GR, SM_DB, SM_DA = 0, 8, 24, 28
N_SM = 32

NT_DIMS = (((1,), (1,)), ((), ()))
TN_DIMS = (((0,), (0,)), ((), ()))

FOX_SCALE = HEAD_DIM ** -0.5
LOG2E = math.log2(math.e)


def _cparams(n_axes, vmem=VMEM_LIMIT):
    return pltpu.CompilerParams(dimension_semantics=("arbitrary",) * n_axes,
                                vmem_limit_bytes=vmem)


def _log_sigmoid(x):
    return jnp.minimum(x, 0.0) - jnp.log(1.0 + jnp.exp(-jnp.abs(x)))


def _softplus(x):
    return jnp.maximum(x, 0.0) + jnp.log(1.0 + jnp.exp(-jnp.abs(x)))


def _sigmoid(x):
    return 1.0 / (1.0 + jnp.exp(-x))


def _silu(x):
    return x * _sigmoid(x)


def _iota(shape, axis):
    return lax.broadcasted_iota(jnp.int32, shape, axis)


def _proj_kernel(x_ref, g_ref, w_ref, wst_ref, o_ref, st_ref, h_scr):
    @pl.when(pl.program_id(1) == 0)
    def _():
        x = x_ref[...]
        ms = jnp.mean(x * x, axis=-1, keepdims=True)
        hb = (x * lax.rsqrt(ms + EPS) * g_ref[...]).astype(BF16)
        h_scr[...] = hb
        st_ref[...] = lax.dot_general(wst_ref[...], hb, NT_DIMS, preferred_element_type=F32)

    o_ref[...] = jnp.dot(h_scr[...], w_ref[...], preferred_element_type=F32)


def _proj_in(x, g, wp, wst, tm):
    rows = x.shape[0]
    tn = 1024
    return pl.pallas_call(
        _proj_kernel,
        grid=(rows // tm, N_PACK // tn),
        in_specs=[pl.BlockSpec((tm, D_MODEL), lambda i, j: (i, 0)),
                  pl.BlockSpec((1, D_MODEL), lambda i, j: (0, 0)),
                  pl.BlockSpec((D_MODEL, tn), lambda i, j: (0, j)),
                  pl.BlockSpec((N_SM, D_MODEL), lambda i, j: (0, 0))],
        out_specs=[pl.BlockSpec((tm, tn), lambda i, j: (i, j)),
                   pl.BlockSpec((N_SM, tm), lambda i, j: (0, i))],
        out_shape=[SDS((rows, N_PACK), F32), SDS((N_SM, rows), F32)],
        scratch_shapes=[pltpu.VMEM((tm, D_MODEL), BF16)],
        compiler_params=_cparams(2),
        name="proj_in",
    )(x, g, wp, wst)


def _fox_prep_kernel(fq_ref, fk_ref, fv_ref, st_ref, qg_ref, kg_ref, fb_ref,
                     qb_ref, k32_ref, kb_ref, vb_ref, lf_ref, cf_ref, carry):
    @pl.when(pl.program_id(0) == 0)
    def _():
        carry[...] = jnp.zeros_like(carry)

    tm = fq_ref.shape[0]
    for h in range(H_FOX):
        sl = slice(h * HEAD_DIM, (h + 1) * HEAD_DIM)
        xq = fq_ref[:, sl]
        qn = xq * lax.rsqrt(jnp.mean(xq * xq, axis=-1, keepdims=True) + EPS) * qg_ref[...]
        qb_ref[:, sl] = (qn * (FOX_SCALE * LOG2E)).astype(BF16)
        xk = fk_ref[:, sl]
        kn = xk * lax.rsqrt(jnp.mean(xk * xk, axis=-1, keepdims=True) + EPS) * kg_ref[...]
        k32_ref[:, sl] = kn
        kb_ref[:, sl] = kn.astype(BF16)
    vb_ref[...] = fv_ref[...].astype(BF16)

    lf = _log_sigmoid(st_ref[SM_FF:SM_FF + H_FOX, :] + fb_ref[...])
    lf_ref[...] = lf
    upper = (_iota((tm, tm), 0) <= _iota((tm, tm), 1)).astype(F32)
    cf = jnp.dot(lf, upper, precision=HI, preferred_element_type=F32) + carry[:, 0:1]
    cf_ref[...] = cf
    carry[...] = jnp.broadcast_to(cf[:, tm - 1:tm], carry.shape)


def _fox_prep(proj, st, qg, kg, fb, tm):
    rows = proj.shape[0]
    blk = lambda c: pl.BlockSpec((tm, D_FOX), lambda i, c=c: (i, c))
    row_spec = pl.BlockSpec((tm, D_FOX), lambda i: (i, 0))
    t_spec = pl.BlockSpec((H_FOX, tm), lambda i: (0, i))
    return pl.pallas_call(
        _fox_prep_kernel,
        grid=(rows // tm,),
        in_specs=[blk(OFF_FQ // D_FOX), blk(OFF_FK // D_FOX), blk(OFF_FV // D_FOX),
                  pl.BlockSpec((N_SM, tm), lambda i: (0, i)),
                  pl.BlockSpec((1, HEAD_DIM), lambda i: (0, 0)),
                  pl.BlockSpec((1, HEAD_DIM), lambda i: (0, 0)),
                  pl.BlockSpec((H_FOX, 1), lambda i: (0, 0))],
        out_specs=[row_spec, row_spec, row_spec, row_spec, t_spec, t_spec],
        out_shape=[SDS((rows, D_FOX), BF16), SDS((rows, D_FOX), F32), SDS((rows, D_FOX), BF16),
                   SDS((rows, D_FOX), BF16), SDS((H_FOX, rows), F32), SDS((H_FOX, rows), F32)],
        scratch_shapes=[pltpu.VMEM((H_FOX, LANES), F32)],
        compiler_params=_cparams(1),
        name="fox_prep",
    )(proj, proj, proj, st, qg, kg, fb)


def _fox_attn_kernel(qi_ref, kj_ref, q_ref, k_ref, v_ref, cf_ref, o_ref, m_scr, l_scr, acc_scr):
    h = pl.program_id(0)
    t = pl.program_id(1)
    qi = qi_ref[t]
    kj = kj_ref[t]
    blk = q_ref.shape[0]

    @pl.when(kj == 0)
    def _():
        m_scr[...] = jnp.full_like(m_scr, -jnp.inf)
        l_scr[...] = jnp.zeros_like(l_scr)
        acc_scr[...] = jnp.zeros_like(acc_scr)

    def step(masked):
        s = lax.dot_general(q_ref[...], k_ref[...], NT_DIMS, preferred_element_type=F32)
        s = s - cf_ref[pl.ds(h, 1), :] * LOG2E
        if masked:
            s = jnp.where(_iota((blk, blk), 1) <= _iota((blk, blk), 0), s, -jnp.inf)
        m_prev = m_scr[...]
        m_new = jnp.maximum(m_prev, jnp.max(s, axis=1, keepdims=True))
        alpha = jnp.exp2(m_prev - m_new)
        p = jnp.exp2(s - m_new)
        l_scr[...] = alpha * l_scr[...] + jnp.sum(p, axis=1, keepdims=True)
        acc_scr[...] = alpha * acc_scr[...] + jnp.dot(p.astype(BF16), v_ref[...],
                                                      preferred_element_type=F32)
        m_scr[...] = m_new

    @pl.when(kj < qi)
    def _():
        step(False)

    @pl.when(kj == qi)
    def _():
        step(True)
        o_ref[...] = acc_scr[...] / l_scr[...]


def _fox_attn(qb, kb, vb, cf, blk):
    rows = qb.shape[0]
    nb = rows // blk
    pairs = [(i, j) for i in range(nb) for j in range(i + 1)]
    qi = jnp.asarray([p[0] for p in pairs], jnp.int32)
    kj = jnp.asarray([p[1] for p in pairs], jnp.int32)
    grid_spec = pltpu.PrefetchScalarGridSpec(
        num_scalar_prefetch=2,
        grid=(H_FOX, len(pairs)),
        in_specs=[pl.BlockSpec((blk, HEAD_DIM), lambda h, t, qi, kj: (qi[t], h)),
                  pl.BlockSpec((blk, HEAD_DIM), lambda h, t, qi, kj: (kj[t], h)),
                  pl.BlockSpec((blk, HEAD_DIM), lambda h, t, qi, kj: (kj[t], h)),
                  pl.BlockSpec((H_FOX, blk), lambda h, t, qi, kj: (0, kj[t]))],
        out_specs=pl.BlockSpec((blk, HEAD_DIM), lambda h, t, qi, kj: (qi[t], h)),
        scratch_shapes=[pltpu.VMEM((blk, 1), F32), pltpu.VMEM((blk, 1), F32),
                        pltpu.VMEM((blk, HEAD_DIM), F32)],
    )
    return pl.pallas_call(
        _fox_attn_kernel,
        grid_spec=grid_spec,
        out_shape=SDS((rows, D_FOX), F32),
        compiler_params=_cparams(2),
        name="fox_attn",
    )(qi, kj, qb, kb, vb, cf)


def _gla_kernel(qk_ref, v_ref, sm_ref, wr_ref, br_ref, o_ref, st_ref, s_scr, *, sub, nsub):
    @pl.when(pl.program_id(0) == 0)
    def _():
        s_scr[...] = jnp.zeros_like(s_scr)

    x = jnp.dot(sm_ref[...], wr_ref[...], precision=HI, preferred_element_type=F32) + br_ref[...]
    log_a = _log_sigmoid(x) * (1.0 / GLA_TAU)

    lsub = int(math.log2(sub))
    lower = (_iota((sub, sub), 1) <= _iota((sub, sub), 0)).astype(F32)
    s_idx = _iota((sub, D_GLA_K), 0)
    ind = ((_iota((D_GLA_K, D_GLA_V), 0) >> 6) == (_iota((D_GLA_K, D_GLA_V), 1) >> 7)).astype(BF16)
    grp = ((_iota((sub, sub * sub), 1) >> lsub) == _iota((sub, sub * sub), 0)).astype(BF16)
    diag = (_iota((D_GLA_V, D_GLA_K), 0) >> 7) == (_iota((D_GLA_V, D_GLA_K), 1) >> 6)

    for j in range(nsub):
        rs = slice(j * sub, (j + 1) * sub)
        b = jnp.dot(lower, log_a[rs], precision=HI, preferred_element_type=F32)
        q = qk_ref[rs, 0:D_GLA_K] * (DK_GLA ** -0.5)
        k = qk_ref[rs, D_GLA_K:2 * D_GLA_K]
        v = v_ref[rs, :]
        state = s_scr[...]
        o_inter = lax.dot_general((q * jnp.exp(b)).astype(BF16), state.astype(BF16), NT_DIMS,
                                  preferred_element_type=F32)
        rows = []
        for t in range(sub):
            dec = jnp.exp(jnp.where(s_idx <= t, b[t:t + 1] - b, -jnp.inf))
            rows.append(dec * q[t:t + 1] * k)
        e = jnp.concatenate(rows, axis=0).astype(BF16)
        a_exp = jnp.dot(e, ind, preferred_element_type=F32)
        av = (a_exp * jnp.concatenate([v] * sub, axis=0)).astype(BF16)
        o_intra = jnp.dot(grp, av, preferred_element_type=F32)
        o_ref[rs, :] = o_inter + o_intra

        b_end = b[sub - 1:sub]
        k_hat = (k * jnp.exp(b_end - b)).astype(BF16)
        upd = lax.dot_general(v.astype(BF16), k_hat, TN_DIMS, preferred_element_type=F32)
        s_scr[...] = state * jnp.exp(b_end) + jnp.where(diag, upd, 0.0)

    @pl.when(pl.program_id(0) == pl.num_programs(0) - 1)
    def _():
        st_ref[...] = s_scr[...]


def _gla_prompt(proj, wr, br, rows_blk, sub):
    rows = proj.shape[0]
    nsub = rows_blk // sub
    return pl.pallas_call(
        functools.partial(_gla_kernel, sub=sub, nsub=nsub),
        grid=(rows // rows_blk,),
        in_specs=[pl.BlockSpec((rows_blk, 2 * D_GLA_K), lambda i: (i, OFF_GQK // (2 * D_GLA_K))),
                  pl.BlockSpec((rows_blk, D_GLA_V), lambda i: (i, OFF_GV // D_GLA_V)),
                  pl.BlockSpec((rows_blk, LANES), lambda i: (i, OFF_SM // LANES)),
                  pl.BlockSpec((LANES, D_GLA_K), lambda i: (0, 0)),
                  pl.BlockSpec((1, D_GLA_K), lambda i: (0, 0))],
        out_specs=[pl.BlockSpec((rows_blk, D_GLA_V), lambda i: (i, 0)),
                   pl.BlockSpec((D_GLA_V, D_GLA_K), lambda i: (0, 0))],
        out_shape=[SDS((rows, D_GLA_V), F32), SDS((D_GLA_V, D_GLA_K), F32)],
        scratch_shapes=[pltpu.VMEM((D_GLA_V, D_GLA_K), F32)],
        compiler_params=_cparams(1),
        name="gla_prompt",
    )(proj, proj, proj, wr, br)


def _l2n(x):
    return x * lax.rsqrt(jnp.sum(x * x, axis=-1, keepdims=True) + EPS)


def _gdn_kernel(dq_ref, dk_ref, dv_ref, sm_ref, smt_ref, cw_ref, alr_ref, dtr_ref, alc_ref, dtc_ref,
                o_ref, st_ref, s_scr, tail_scr, *, nchunk):
    @pl.when(pl.program_id(0) == 0)
    def _():
        s_scr[...] = jnp.zeros_like(s_scr)
        tail_scr[...] = jnp.zeros_like(tail_scr)

    rows = dq_ref.shape[0]
    c = GDN_CHUNK

    def conv(x_ref, part):
        x = x_ref[...]
        xe = jnp.concatenate([tail_scr[part], x], axis=0)
        y = None
        for i in range(CONV_W):
            lo = SUBLANES - (CONV_W - 1) + i
            term = xe[lo:lo + rows] * cw_ref[i:i + 1, part * D_GDN:(part + 1) * D_GDN]
            y = term if y is None else y + term
        tail_scr[part] = x[rows - SUBLANES:rows]
        return _silu(y)

    cq = conv(dq_ref, 0)
    ck = conv(dk_ref, 1)
    cv = conv(dv_ref, 2)

    sm = sm_ref[...]
    beta_c = _sigmoid(sm)
    g_c = -jnp.exp(alr_ref[...]) * _softplus(sm + dtr_ref[...])
    g_r = -jnp.exp(alc_ref[...]) * _softplus(smt_ref[...] + dtc_ref[...])

    lower = (_iota((c, c), 1) <= _iota((c, c), 0))
    strict = (_iota((c, c), 1) < _iota((c, c), 0))
    lower_f = lower.astype(F32)
    upper_f = (_iota((c, c), 0) <= _iota((c, c), 1)).astype(F32)
    eye = (_iota((c, c), 0) == _iota((c, c), 1)).astype(F32)

    def mm(a, b):
        return jnp.dot(a, b, precision=HI, preferred_element_type=F32)

    for ci in range(nchunk):
        rs = slice(ci * c, (ci + 1) * c)
        gam_c = mm(lower_f, g_c[rs])
        gam_r = mm(g_r[:, rs], upper_f)
        for h in range(H_GDN):
            hs = slice(h * HEAD_DIM, (h + 1) * HEAD_DIM)
            qh = _l2n(cq[rs, hs]) * (HEAD_DIM ** -0.5)
            kh = _l2n(ck[rs, hs])
            vh = cv[rs, hs]
            gc = gam_c[:, SM_DA + h:SM_DA + h + 1]
            gr = gam_r[SM_DA + h:SM_DA + h + 1, :]
            bc = beta_c[rs, SM_DB + h:SM_DB + h + 1]
            decay = jnp.exp(jnp.where(lower, gc - gr, -jnp.inf))
            kk = lax.dot_general(kh, kh, NT_DIMS, precision=HI, preferred_element_type=F32)
            a = jnp.where(strict, bc * kk * decay, 0.0)
            pw = -a
            inv = eye + pw
            for _ in range(int(math.log2(c)) - 1):
                pw = mm(pw, pw)
                inv = inv + mm(inv, pw)
            rhs = jnp.concatenate([bc * vh, (bc * jnp.exp(gc)) * kh], axis=1)
            sol = mm(inv, rhs)
            u = sol[:, :HEAD_DIM]
            w = sol[:, HEAD_DIM:]
            state = s_scr[h]
            delta = u - mm(w, state)
            attn = lax.dot_general(qh, kh, NT_DIMS, precision=HI, preferred_element_type=F32) * decay
            o_ref[rs, hs] = mm(qh * jnp.exp(gc), state) + mm(attn, delta)
            g_end = gc[c - 1:c]
            s_scr[h] = jnp.exp(g_end) * state + lax.dot_general(
                kh * jnp.exp(g_end - gc), delta, TN_DIMS, precision=HI, preferred_element_type=F32)

    @pl.when(pl.program_id(0) == pl.num_programs(0) - 1)
    def _():
        st_ref[...] = s_scr[...]


def _gdn_prompt(proj, st, cw, alr, dtr, alc, dtc, rows_blk):
    rows = proj.shape[0]
    blk = lambda off: pl.BlockSpec((rows_blk, D_GDN), lambda i, off=off: (i, off // D_GDN))
    full = lambda shape: pl.BlockSpec(shape, lambda i: (0,) * len(shape))
    return pl.pallas_call(
        functools.partial(_gdn_kernel, nchunk=rows_blk // GDN_CHUNK),
        grid=(rows // rows_blk,),
        in_specs=[blk(OFF_DQ), blk(OFF_DK), blk(OFF_DV),
                  pl.BlockSpec((rows_blk, LANES), lambda i: (i, OFF_SM // LANES)),
                  pl.BlockSpec((N_SM, rows_blk), lambda i: (0, i)),
                  full((CONV_W, 3 * D_GDN)), full((1, LANES)), full((1, LANES)),
                  full((N_SM, 1)), full((N_SM, 1))],
        out_specs=[pl.BlockSpec((rows_blk, D_GDN), lambda i: (i, 0)),
                   full((H_GDN, HEAD_DIM, HEAD_DIM))],
        out_shape=[SDS((rows, D_GDN), F32), SDS((H_GDN, HEAD_DIM, HEAD_DIM), F32)],
        scratch_shapes=[pltpu.VMEM((H_GDN, HEAD_DIM, HEAD_DIM), F32),
                        pltpu.VMEM((3, SUBLANES, D_GDN), F32)],
        compiler_params=_cparams(1),
        name="gdn_prompt",
    )(proj, proj, proj, proj, st, cw, alr, dtr, alc, dtc)


def _merge_kernel(of_ref, og_ref, od_ref, z_ref, x_ref, g_ref, w_ref, y_ref):
    parts = []
    for h in range(H_ALL):
        if h < H_FOX:
            o = of_ref[:, h * HEAD_DIM:(h + 1) * HEAD_DIM]
        elif h < H_FOX + H_GLA:
            o = og_ref[:, (h - H_FOX) * HEAD_DIM:(h - H_FOX + 1) * HEAD_DIM]
        else:
            o = od_ref[:, (h - H_FOX - H_GLA) * HEAD_DIM:(h - H_FOX - H_GLA + 1) * HEAD_DIM]
        sl = slice(h * HEAD_DIM, (h + 1) * HEAD_DIM)
        n = o * lax.rsqrt(jnp.mean(o * o, axis=-1, keepdims=True) + EPS) * g_ref[:, sl]
        parts.append((n * _silu(z_ref[:, sl])).astype(BF16))
    mixed = jnp.concatenate(parts, axis=1)
    y_ref[...] = x_ref[...] + jnp.dot(mixed, w_ref[...], preferred_element_type=F32)


def _merge(o_fox, o_gla, o_gdn, proj, x, g, w_out, tm):
    rows = x.shape[0]
    return pl.pallas_call(
        _merge_kernel,
        grid=(rows // tm,),
        in_specs=[pl.BlockSpec((tm, D_FOX), lambda i: (i, 0)),
                  pl.BlockSpec((tm, D_GLA_V), lambda i: (i, 0)),
                  pl.BlockSpec((tm, D_GDN), lambda i: (i, 0)),
                  pl.BlockSpec((tm, D_MIX), lambda i: (i, OFF_Z // D_MIX)),
                  pl.BlockSpec((tm, D_MODEL), lambda i: (i, 0)),
                  pl.BlockSpec((1, D_MIX), lambda i: (0, 0)),
                  pl.BlockSpec((D_MIX, D_MODEL), lambda i: (0, 0))],
        out_specs=pl.BlockSpec((tm, D_MODEL), lambda i: (i, 0)),
        out_shape=SDS((rows, D_MODEL), F32),
        compiler_params=_cparams(1),
        name="merge_out",
    )(o_fox, o_gla, o_gdn, proj, x, g, w_out)


def _sample_prep_kernel(proj_ref, conv_ref, qg_ref, kg_ref, fb_ref, wr_ref, br_ref, cw_ref,
                        alr_ref, dtr_ref,
                        qn_ref, kn_ref, sm_ref, ga_ref, gq_ref, dq_ref, dk_ref, dv_ref):
    for h in range(H_FOX):
        sl = slice(h * HEAD_DIM, (h + 1) * HEAD_DIM)
        xq = proj_ref[:, OFF_FQ + h * HEAD_DIM:OFF_FQ + (h + 1) * HEAD_DIM]
        qn = xq * lax.rsqrt(jnp.mean(xq * xq, axis=-1, keepdims=True) + EPS) * qg_ref[...]
        qn_ref[:, sl] = qn * FOX_SCALE
        xk = proj_ref[:, OFF_FK + h * HEAD_DIM:OFF_FK + (h + 1) * HEAD_DIM]
        kn_ref[:, sl] = xk * lax.rsqrt(jnp.mean(xk * xk, axis=-1, keepdims=True) + EPS) * kg_ref[...]

    sm = proj_ref[:, OFF_SM:OFF_SM + LANES]
    lane = _iota(sm.shape, 1)
    logf = _log_sigmoid(sm + fb_ref[...])
    beta = _sigmoid(sm)
    decay = jnp.exp(-jnp.exp(alr_ref[...]) * _softplus(sm + dtr_ref[...]))
    sm_ref[...] = jnp.where(lane < SM_GR, logf, jnp.where(lane < SM_DA, beta, decay))

    x = jnp.dot(sm, wr_ref[...], precision=HI, preferred_element_type=F32) + br_ref[...]
    ga_ref[...] = jnp.exp(_log_sigmoid(x) * (1.0 / GLA_TAU))
    gq_ref[...] = proj_ref[:, OFF_GQK:OFF_GQK + D_GLA_K] * (DK_GLA ** -0.5)

    def conv(part):
        off = OFF_DQ + part * D_GDN
        y = proj_ref[:, off:off + D_GDN] * cw_ref[CONV_W - 1:CONV_W, part * D_GDN:(part + 1) * D_GDN]
        for i in range(CONV_W - 1):
            y = y + conv_ref[i][:, part * D_GDN:(part + 1) * D_GDN] * \
                cw_ref[i:i + 1, part * D_GDN:(part + 1) * D_GDN]
        return _silu(y)

    cq, ck, cv = conv(0), conv(1), conv(2)
    for h in range(H_GDN):
        sl = slice(h * HEAD_DIM, (h + 1) * HEAD_DIM)
        dq_ref[:, sl] = _l2n(cq[:, sl]) * (HEAD_DIM ** -0.5)
        dk_ref[:, sl] = _l2n(ck[:, sl])
    dv_ref[...] = cv


def _sample_prep(proj, conv_t, qg, kg, fb_row, wr, br, cw, alr, dtr):
    b = proj.shape[0]
    shapes = [SDS((b, D_FOX), F32), SDS((b, D_FOX), F32), SDS((b, LANES), F32),
              SDS((b, D_GLA_K), F32), SDS((b, D_GLA_K), F32),
              SDS((b, D_GDN), F32), SDS((b, D_GDN), F32), SDS((b, D_GDN), F32)]
    return pl.pallas_call(
        _sample_prep_kernel,
        out_shape=shapes,
        compiler_params=pltpu.CompilerParams(vmem_limit_bytes=VMEM_LIMIT),
        name="sample_prep",
    )(proj, conv_t, qg, kg, fb_row, wr, br, cw, alr, dtr)


def _fox_bias_kernel(pt_ref, *refs, n_pages):
    lf_refs = refs[:n_pages]
    lfn_ref, o_ref = refs[n_pages], refs[n_pages + 1]
    later = (_iota((PAGE, PAGE), 0) > _iota((PAGE, PAGE), 1)).astype(F32)
    carry = lfn_ref[0]
    for i in reversed(range(n_pages)):
        lf = lf_refs[i][...]
        o_ref[0, :, i * PAGE:(i + 1) * PAGE] = jnp.dot(lf, later, precision=HI,
                                                       preferred_element_type=F32) + carry
        carry = carry + jnp.sum(lf, axis=1, keepdims=True)


def _fox_bias(page_flat, logf_t, layer, lfn3, n_pages):
    b = lfn3.shape[0]
    in_specs = [pl.BlockSpec((None, None, H_FOX, PAGE),
                             lambda bi, pt, i=i: (layer, pt[bi * n_pages + i], 0, 0))
                for i in range(n_pages)]
    in_specs.append(pl.BlockSpec((1, H_FOX, LANES), lambda bi, pt: (bi, 0, 0)))
    grid_spec = pltpu.PrefetchScalarGridSpec(
        num_scalar_prefetch=1, grid=(b,), in_specs=in_specs,
        out_specs=pl.BlockSpec((1, H_FOX, n_pages * PAGE), lambda bi, pt: (bi, 0, 0)))
    return pl.pallas_call(
        functools.partial(_fox_bias_kernel, n_pages=n_pages),
        grid_spec=grid_spec,
        out_shape=SDS((b, H_FOX, n_pages * PAGE), F32),
        compiler_params=_cparams(1),
        name="fox_bias",
    )(page_flat, *([logf_t] * n_pages), lfn3)


def _fox_decode_kernel(pt_ref, *refs, ppc):
    k_refs = refs[:ppc]
    v_refs = refs[ppc:2 * ppc]
    bias_ref, q_ref, kn_ref, vn_ref, o_ref, m_scr, l_scr, acc_scr = refs[2 * ppc:]
    c = pl.program_id(1)

    @pl.when(c == 0)
    def _():
        m_scr[...] = jnp.full_like(m_scr, -jnp.inf)
        l_scr[...] = jnp.zeros_like(l_scr)
        acc_scr[...] = jnp.zeros_like(acc_scr)

    q = q_ref[0]
    qb = q.astype(BF16)
    row = _iota((H_FOX, PAGE), 0)

    def head_rows(ref, h):
        return ref[pl.ds(h, PAGE, stride=H_FOX), :].astype(BF16)

    s_pages = []
    for i in range(ppc):
        s = jnp.zeros((H_FOX, PAGE), F32)
        for h in range(H_FOX):
            r = lax.dot_general(qb, head_rows(k_refs[i], h), NT_DIMS, preferred_element_type=F32)
            s = jnp.where(row == h, r, s)
        s_pages.append(s)
    s = jnp.concatenate(s_pages, axis=1) + bias_ref[0]

    m_prev = m_scr[...]
    m_new = jnp.maximum(m_prev, jnp.max(s, axis=1, keepdims=True))
    alpha = jnp.exp(m_prev - m_new)
    p = jnp.exp(s - m_new)
    l_scr[...] = alpha * l_scr[...] + jnp.sum(p, axis=1, keepdims=True)
    m_scr[...] = m_new
    pb = p.astype(BF16)
    acc = alpha * acc_scr[...]
    for i in range(ppc):
        pi = pb[:, i * PAGE:(i + 1) * PAGE]
        for h in range(H_FOX):
            r = jnp.dot(pi, head_rows(v_refs[i], h), preferred_element_type=F32)
            acc = acc + jnp.where(row == h, r, 0.0)
    acc_scr[...] = acc

    @pl.when(c == pl.num_programs(1) - 1)
    def _():
        s_new = jnp.sum(q * kn_ref[0], axis=1, keepdims=True)
        m_old = m_scr[...]
        m_fin = jnp.maximum(m_old, s_new)
        a = jnp.exp(m_old - m_fin)
        p_new = jnp.exp(s_new - m_fin)
        l_fin = a * l_scr[...] + p_new
        o_ref[0] = (a * acc_scr[...] + p_new * vn_ref[0]) / l_fin


def _fox_decode(page_flat, cache_k, cache_v, layer, bias, q3, kn3, vn3, n_pages, ppc):
    b = q3.shape[0]
    nchunks = n_pages // ppc

    def page_spec(i):
        return pl.BlockSpec((None, None, PAGE * H_FOX, HEAD_DIM),
                            lambda bi, c, pt, i=i: (layer, pt[bi * n_pages + c * ppc + i], 0, 0))

    tok = pl.BlockSpec((1, H_FOX, HEAD_DIM), lambda bi, c, pt: (bi, 0, 0))
    grid_spec = pltpu.PrefetchScalarGridSpec(
        num_scalar_prefetch=1, grid=(b, nchunks),
        in_specs=[page_spec(i) for i in range(ppc)] + [page_spec(i) for i in range(ppc)] +
                 [pl.BlockSpec((1, H_FOX, ppc * PAGE), lambda bi, c, pt: (bi, 0, c)), tok, tok, tok],
        out_specs=tok,
        scratch_shapes=[pltpu.VMEM((H_FOX, 1), F32), pltpu.VMEM((H_FOX, 1), F32),
                        pltpu.VMEM((H_FOX, HEAD_DIM), F32)])
    return pl.pallas_call(
        functools.partial(_fox_decode_kernel, ppc=ppc),
        grid_spec=grid_spec,
        out_shape=SDS((b, H_FOX, HEAD_DIM), F32),
        compiler_params=_cparams(2),
        name="fox_decode",
    )(page_flat, *([cache_k] * ppc), *([cache_v] * ppc), bias, q3, kn3, vn3)


def _head_rows(x8, n_heads, rows_per_head):
    return jnp.concatenate([jnp.broadcast_to(x8[h:h + 1], (rows_per_head, x8.shape[1]))
                            for h in range(n_heads)], axis=0)


def _head_block_rows(x_row, per_head):
    n = x_row.shape[1]
    shift = int(math.log2(per_head))
    keep = (_iota((SUBLANES, n), 1) >> shift) == _iota((SUBLANES, n), 0)
    return jnp.where(keep, jnp.broadcast_to(x_row, (SUBLANES, n)), 0.0)


def _gla_decode_kernel(s_ref, a_ref, k_ref, v_ref, q_ref, so_ref, o_ref):
    for b in range(s_ref.shape[0]):
        v_full = _head_rows(v_ref[b], H_GLA, DK_GLA)
        s_new = a_ref[b] * s_ref[b] + k_ref[b] * v_full
        so_ref[b] = s_new
        q_blk = _head_block_rows(q_ref[b:b + 1, :], DK_GLA)
        o_ref[b] = jnp.dot(q_blk.astype(BF16), s_new.astype(BF16), preferred_element_type=F32)


def _gla_decode(state, a_col, k_col, v8, q, bb):
    b = state.shape[0]
    big = pl.BlockSpec((bb, D_GLA_K, HEAD_DIM), lambda i: (i, 0, 0))
    small = pl.BlockSpec((bb, SUBLANES, HEAD_DIM), lambda i: (i, 0, 0))
    return pl.pallas_call(
        _gla_decode_kernel,
        grid=(b // bb,),
        in_specs=[big, big, big, small, pl.BlockSpec((bb, D_GLA_K), lambda i: (i, 0))],
        out_specs=[big, small],
        out_shape=[SDS((b, D_GLA_K, HEAD_DIM), F32), SDS((b, SUBLANES, HEAD_DIM), F32)],
        compiler_params=_cparams(1),
        name="gla_decode",
    )(state, a_col, k_col, v8, q)


def _gdn_decode_kernel(s_ref, kc_ref, k_ref, q_ref, v_ref, beta_ref, dec_ref, so_ref, o_ref):
    for b in range(s_ref.shape[0]):
        state = s_ref[b]
        k_blk = _head_block_rows(k_ref[b:b + 1, :], HEAD_DIM)
        sk = jnp.dot(k_blk.astype(BF16), state.astype(BF16), preferred_element_type=F32)
        dec8 = dec_ref[b]
        delta = beta_ref[b] * (v_ref[b] - dec8 * sk)
        s_new = _head_rows(dec8, H_GDN, HEAD_DIM) * state + kc_ref[b] * _head_rows(delta, H_GDN, HEAD_DIM)
        so_ref[b] = s_new
        q_blk = _head_block_rows(q_ref[b:b + 1, :], HEAD_DIM)
        o_ref[b] = jnp.dot(q_blk.astype(BF16), s_new.astype(BF16), preferred_element_type=F32)


def _gdn_decode(state, k_col, k, q, v8, beta8, dec8, bb):
    b = state.shape[0]
    big = pl.BlockSpec((bb, D_GDN, HEAD_DIM), lambda i: (i, 0, 0))
    small = pl.BlockSpec((bb, SUBLANES, HEAD_DIM), lambda i: (i, 0, 0))
    row = pl.BlockSpec((bb, D_GDN), lambda i: (i, 0))
    return pl.pallas_call(
        _gdn_decode_kernel,
        grid=(b // bb,),
        in_specs=[big, big, row, row, small, small, small],
        out_specs=[big, small],
        out_shape=[SDS((b, D_GDN, HEAD_DIM), F32), SDS((b, SUBLANES, HEAD_DIM), F32)],
        compiler_params=_cparams(1),
        name="gdn_decode",
    )(state, k_col, k, q, v8, beta8, dec8)


def _pack_w_in(w):
    smalls = jnp.concatenate([w[:, 3072:3080], w[:, 4104:4120], w[:, 5656:5664]], axis=1)
    packed = jnp.concatenate([
        w[:, 0:3072],
        w[:, 3080:3592],
        w[:, 3592:4104],
        w[:, 5664:7712],
        w[:, 4120:5656],
        smalls,
        jnp.zeros((w.shape[0], N_PACK - OFF_SM - N_SM), w.dtype)], axis=1)
    return packed.astype(BF16), smalls.T.astype(BF16)


def _lane_param(vec, off):
    return jnp.zeros((1, LANES), F32).at[0, off:off + vec.shape[0]].set(vec)


def _rows8(x, n_heads):
    b = x.shape[0]
    x = x.reshape(b, n_heads, HEAD_DIM)
    return jnp.pad(x, ((0, 0), (0, SUBLANES - n_heads), (0, 0)))


def _lanes8(x):
    b, n = x.shape
    x = jnp.pad(x, ((0, 0), (0, SUBLANES - n)))
    return jnp.broadcast_to(x[:, :, None], (b, SUBLANES, LANES))


def _col(x):
    return jnp.broadcast_to(x[:, :, None], x.shape + (LANES,))


def kernel(x_prompt, x_sample, cache_k, cache_v, cache_logf, state_conv, state_gla, state_gdn,
           page_table, norm_g, w_in, w_out, fox_qnorm_g, fox_knorm_g, fox_f_bias, gla_w_r2, gla_b_r,
           gdn_conv_w, gdn_a_log, gdn_dt_bias, out_norm_g):
    depth = w_in.shape[0]
    bp, lp, _ = x_prompt.shape
    assert bp == 1
    bs = x_sample.shape[0]
    n_pool = cache_k.shape[1]
    n_pages = page_table.shape[1]
    page_flat = page_table.reshape(-1).astype(jnp.int32)
    ck4 = cache_k.reshape(depth, n_pool, PAGE * H_FOX, HEAD_DIM)
    cv4 = cache_v.reshape(depth, n_pool, PAGE * H_FOX, HEAD_DIM)
    logf_t = jnp.swapaxes(cache_logf, 2, 3)

    tm_p = min(1024, lp)
    tm_merge = min(512, lp)
    blk_attn = min(1024, lp)
    tm_prep = min(512, lp)
    ppc = min(8, n_pages)
    bb = min(8, bs)

    yp = x_prompt[0]
    ys = x_sample[:, 0]
    outs = {k: [] for k in ("p_k", "p_v", "p_logf", "p_conv", "p_gla", "p_gdn",
                            "s_k", "s_v", "s_logf", "s_conv", "s_gla", "s_gdn")}
    for l in range(depth):
        wp, wst = _pack_w_in(w_in[l])
        w_o = w_out[l].astype(BF16)
        ng = norm_g[l][None]
        qg, kg = fox_qnorm_g[l][None], fox_knorm_g[l][None]
        wr = jnp.zeros((LANES, D_GLA_K), F32).at[SM_GR:SM_GR + GLA_RANK].set(gla_w_r2[l])
        br = gla_b_r[l][None]
        cw = gdn_conv_w[l]
        alr, dtr = _lane_param(gdn_a_log[l], SM_DA), _lane_param(gdn_dt_bias[l], SM_DA)
        alc = jnp.zeros((N_SM, 1), F32).at[SM_DA:SM_DA + H_GDN, 0].set(gdn_a_log[l])
        dtc = jnp.zeros((N_SM, 1), F32).at[SM_DA:SM_DA + H_GDN, 0].set(gdn_dt_bias[l])
        og = out_norm_g[l][None]

        proj, st = _proj_in(yp, ng, wp, wst, tm_p)
        qb, k32, kb, vb, lf_t, cf_t = _fox_prep(proj, st, qg, kg, fox_f_bias[l][:, None], tm_prep)
        o_fox = _fox_attn(qb, kb, vb, cf_t, blk_attn)
        o_gla, gla_st = _gla_prompt(proj, wr, br, min(128, lp), 16)
        o_gdn, gdn_st = _gdn_prompt(proj, st, cw, alr, dtr, alc, dtc, min(128, lp))
        yp_new = _merge(o_fox, o_gla, o_gdn, proj, yp, og, w_o, tm_merge)

        outs["p_k"].append(k32.reshape(1, lp // PAGE, PAGE, H_FOX, HEAD_DIM))
        outs["p_v"].append(proj[:, OFF_FV:OFF_FV + D_FOX].reshape(1, lp // PAGE, PAGE, H_FOX, HEAD_DIM))
        outs["p_logf"].append(lf_t.T.reshape(1, lp // PAGE, PAGE, H_FOX))
        outs["p_conv"].append(proj[lp - (CONV_W - 1):, OFF_DQ:OFF_DQ + 3 * D_GDN][None])
        gla_heads = jnp.stack([gla_st[h * HEAD_DIM:(h + 1) * HEAD_DIM, h * DK_GLA:(h + 1) * DK_GLA].T
                               for h in range(H_GLA)])
        outs["p_gla"].append(gla_heads[None])
        outs["p_gdn"].append(gdn_st[None])
        yp = yp_new

        proj_s, _ = _proj_in(ys, ng, wp, wst, bs)
        conv_t = jnp.swapaxes(state_conv[l], 0, 1)
        fb_row = _lane_param(fox_f_bias[l], SM_FF)
        qn, kn, sm_s, ga, gq, dq, dk, dv = _sample_prep(proj_s, conv_t, qg, kg, fb_row, wr, br, cw, alr, dtr)
        v_new = proj_s[:, OFF_FV:OFF_FV + D_FOX]
        logf_new = sm_s[:, SM_FF:SM_FF + H_FOX]

        bias = _fox_bias(page_flat, logf_t, l, _lanes8(logf_new), n_pages)
        o_fox_s = _fox_decode(page_flat, ck4, cv4, l, bias, qn.reshape(bs, H_FOX, HEAD_DIM),
                              kn.reshape(bs, H_FOX, HEAD_DIM), v_new.reshape(bs, H_FOX, HEAD_DIM),
                              n_pages, ppc).reshape(bs, D_FOX)

        gk_s = proj_s[:, OFF_GQK + D_GLA_K:OFF_GQK + 2 * D_GLA_K]
        gv_s = proj_s[:, OFF_GV:OFF_GV + D_GLA_V]
        gla_new, o_gla8 = _gla_decode(state_gla[l].reshape(bs, D_GLA_K, HEAD_DIM), _col(ga), _col(gk_s),
                                      _rows8(gv_s, H_GLA), gq, bb)
        o_gla_s = o_gla8[:, :H_GLA].reshape(bs, D_GLA_V)

        gdn_new, o_gdn8 = _gdn_decode(state_gdn[l].reshape(bs, D_GDN, HEAD_DIM), _col(dk), dk, dq,
                                      _rows8(dv, H_GDN), _lanes8(sm_s[:, SM_DB:SM_DB + H_GDN]),
                                      _lanes8(sm_s[:, SM_DA:SM_DA + H_GDN]), bb)
        o_gdn_s = o_gdn8[:, :H_GDN].reshape(bs, D_GDN)
        ys_new = _merge(o_fox_s, o_gla_s, o_gdn_s, proj_s, ys, og, w_o, bs)

        outs["s_k"].append(kn.reshape(bs, 1, H_FOX, HEAD_DIM))
        outs["s_v"].append(v_new.reshape(bs, 1, H_FOX, HEAD_DIM))
        outs["s_logf"].append(logf_new.reshape(bs, 1, H_FOX))
        dqkv_s = proj_s[:, OFF_DQ:OFF_DQ + 3 * D_GDN]
        outs["s_conv"].append(jnp.concatenate([state_conv[l][:, 1:], dqkv_s[:, None]], axis=1))
        outs["s_gla"].append(gla_new.reshape(bs, H_GLA, DK_GLA, HEAD_DIM))
        outs["s_gdn"].append(gdn_new.reshape(bs, H_GDN, HEAD_DIM, HEAD_DIM))
        ys = ys_new

    st = {k: jnp.stack(v) for k, v in outs.items()}
    return (yp[None], ys[:, None], st["p_k"], st["p_v"], st["p_logf"], st["p_conv"], st["p_gla"],
            st["p_gdn"], st["s_k"], st["s_v"], st["s_logf"], st["s_conv"], st["s_gla"], st["s_gdn"])
```
